```python
import math
import jax, jax.numpy as jnp
from jax import lax
import numpy as np

D_MODEL = 2048
BATCH = 8
SEQ = 2048
DEPTH = 2

CHUNK = 64
N_A_LAYERS = DEPTH // 2
N_B_LAYERS = DEPTH - N_A_LAYERS
M_HEADS = 8
M_QK_DIM = D_MODEL // (2 * M_HEADS)
M_V_DIM = D_MODEL // M_HEADS
M_CONV = 4
M_QK_WIDTH = M_HEADS * M_QK_DIM
M_V_WIDTH = M_HEADS * M_V_DIM
M_IN_WIDTH = 2 * M_QK_WIDTH + 2 * M_V_WIDTH + 2 * M_HEADS
A_HEADS = 16
NOPE_DIM = 128
ROPE_DIM = 64
V_DIM = 128
KV_RANK = D_MODEL // 4
Q_RANK = D_MODEL // 4
ROPE_THETA = 10000.0
Q_BLOCK = 128
D_FF = 4 * D_MODEL
DN_ALPHA = (2 * DEPTH) ** 0.25
DN_BETA = (8 * DEPTH) ** -0.25
LN_EPS = 1e-5
RMS_EPS = 1e-6

kernel_name = "yoco_mlstm_mla_deepnorm_trunk"

F32 = jnp.float32


def layer_norm(x, g, b):
    xf = x.astype(F32)
    mu = jnp.mean(xf, axis=-1, keepdims=True)
    var = jnp.mean(jnp.square(xf - mu), axis=-1, keepdims=True)
    return ((xf - mu) * lax.rsqrt(var + LN_EPS) * g.astype(F32) + b.astype(F32)).astype(x.dtype)


def rms_norm(x, g):
    xf = x.astype(F32)
    ms = jnp.mean(jnp.square(xf), axis=-1, keepdims=True)
    return (xf * lax.rsqrt(ms + RMS_EPS) * g.astype(F32)).astype(x.dtype)


def rope(x, cos, sin):
    half = x.shape[-1] // 2
    x1, x2 = x[..., :half], x[..., half:]
    return jnp.concatenate([x1 * cos - x2 * sin, x2 * cos + x1 * sin], axis=-1)


def causal_depthwise_conv(x, w, b):
    c = x.shape[-1]
    y = lax.conv_general_dilated(
        x, w[:, None, :].astype(x.dtype), window_strides=(1,), padding=[(M_CONV - 1, 0)],
        dimension_numbers=("NWC", "WIO", "NWC"), feature_group_count=c)
    return y + b.astype(x.dtype)


def mlstm_mixer(x, w_in, b_gates, conv_w, conv_b, norm_w, w_out):
    B, S, _ = x.shape
    NC = S // CHUNK
    proj = x @ w_in
    qk = proj[..., :2 * M_QK_WIDTH]
    v = proj[..., 2 * M_QK_WIDTH:2 * M_QK_WIDTH + M_V_WIDTH]
    o = proj[..., 2 * M_QK_WIDTH + M_V_WIDTH:2 * M_QK_WIDTH + 2 * M_V_WIDTH]
    gates = proj[..., 2 * M_QK_WIDTH + 2 * M_V_WIDTH:].astype(F32) + b_gates.astype(F32)
    qk = jax.nn.silu(causal_depthwise_conv(qk, conv_w, conv_b))
    q = qk[..., :M_QK_WIDTH] * (M_QK_DIM ** -0.5)
    k = qk[..., M_QK_WIDTH:]
    i_pre = gates[..., :M_HEADS]
    log_f = jax.nn.log_sigmoid(gates[..., M_HEADS:])

    def to_chunks(t, d):
        return t.astype(F32).reshape(B, NC, CHUNK, M_HEADS, d).transpose(1, 0, 3, 2, 4)

    def gate_chunks(t):
        return t.reshape(B, NC, CHUNK, M_HEADS).transpose(1, 0, 3, 2)

    causal = jnp.tril(jnp.ones((CHUNK, CHUNK), dtype=bool))

    def step(carry, inp):
        C, n, m = carry
        qc, kc, vc, ic, fc = inp
        bcum = jnp.cumsum(fc, axis=-1)
        g = bcum[..., -1]
        dmat = bcum[..., :, None] - bcum[..., None, :] + ic[..., None, :]
        dmat = jnp.where(causal, dmat, -jnp.inf)
        inter = bcum + m[..., None]
        m_t = jnp.maximum(jnp.max(dmat, axis=-1), inter)
        w_intra = jnp.exp(dmat - m_t[..., None])
        w_inter = jnp.exp(inter - m_t)
        sw = jnp.einsum('bhtd,bhsd->bhts', qc, kc) * w_intra
        num = jnp.einsum('bhts,bhsv->bhtv', sw, vc) \
            + w_inter[..., None] * jnp.einsum('bhtd,bhdv->bhtv', qc, C)
        den = jnp.sum(sw, axis=-1) + w_inter * jnp.einsum('bhtd,bhd->bht', qc, n)
        h = num / jnp.maximum(jnp.abs(den), jnp.exp(-m_t))[..., None]
        a = g[..., None] - bcum + ic
        m_new = jnp.maximum(g + m, jnp.max(a, axis=-1))
        decay = jnp.exp(g + m - m_new)
        wk = jnp.exp(a - m_new[..., None])
        C_new = decay[..., None, None] * C + jnp.einsum('bhs,bhsd,bhsv->bhdv', wk, kc, vc)
        n_new = decay[..., None] * n + jnp.einsum('bhs,bhsd->bhd', wk, kc)
        return (C_new, n_new, m_new), h

    init = (jnp.zeros((B, M_HEADS, M_QK_DIM, M_V_DIM), F32),
            jnp.zeros((B, M_HEADS, M_QK_DIM), F32),
            jnp.zeros((B, M_HEADS), F32))
    xs = (to_chunks(q, M_QK_DIM), to_chunks(k, M_QK_DIM), to_chunks(v, M_V_DIM),
          gate_chunks(i_pre), gate_chunks(log_f))
    _, h = lax.scan(step, init, xs)
    h = h.transpose(1, 0, 3, 2, 4).reshape(B, S, M_HEADS, M_V_DIM)
    mu = jnp.mean(h, axis=-1, keepdims=True)
    var = jnp.mean(jnp.square(h - mu), axis=-1, keepdims=True)
    hn = (h - mu) * lax.rsqrt(var + LN_EPS) * norm_w.astype(F32).reshape(M_HEADS, M_V_DIM)
    hn = hn.reshape(B, S, M_V_WIDTH).astype(x.dtype)
    return (jax.nn.sigmoid(o) * hn) @ w_out


def shared_latent_kv(x, w_down, norm_w, w_up, cos, sin):
    B, S, _ = x.shape
    ckv = x @ w_down
    c = rms_norm(ckv[..., :KV_RANK], norm_w)
    k_rope = rope(ckv[..., KV_RANK:], cos, sin)
    kv = (c @ w_up).reshape(B, S, A_HEADS, NOPE_DIM + V_DIM)
    return kv[..., :NOPE_DIM], k_rope, kv[..., NOPE_DIM:]


def mla_mixer(x, k_nope, k_rope, v, w_dq, q_norm_w, w_uq, w_out, cos, sin):
    B, S, _ = x.shape
    q = (rms_norm(x @ w_dq, q_norm_w) @ w_uq).reshape(B, S, A_HEADS, NOPE_DIM + ROPE_DIM)
    q_nope = q[..., :NOPE_DIM]
    q_rope = rope(q[..., NOPE_DIM:], cos[:, None, :], sin[:, None, :])
    scale = (NOPE_DIM + ROPE_DIM) ** -0.5
    chunk_id = jnp.arange(S) // CHUNK
    outs = []
    for blk in range(S // Q_BLOCK):
        qs, qe = blk * Q_BLOCK, (blk + 1) * Q_BLOCK
        s = (jnp.einsum('bqhd,bkhd->bhqk', q_nope[:, qs:qe], k_nope[:, :qe])
             + jnp.einsum('bqhr,bkr->bhqk', q_rope[:, qs:qe], k_rope[:, :qe])).astype(F32) * scale
        mask = chunk_id[qs:qe, None] >= chunk_id[None, :qe]
        p = jax.nn.softmax(jnp.where(mask, s, -jnp.inf), axis=-1).astype(v.dtype)
        outs.append(jnp.einsum('bhqk,bkhv->bqhv', p, v[:, :qe]))
    o = jnp.concatenate(outs, axis=1).reshape(B, S, A_HEADS * V_DIM)
    return o @ w_out


def squared_relu_mlp(x, w1, w2):
    return jnp.square(jax.nn.relu(x @ w1)) @ w2


def setup_inputs(seed: int = 0) -> dict:
    key = jax.random.key(seed)
    ks = jax.random.split(key, 21)

    def nrm(k, shape, scale):
        return jax.random.normal(k, shape, F32) * scale

    x = nrm(ks[0], (BATCH, SEQ, D_MODEL), 1.0)
    a_w_in = nrm(ks[1], (N_A_LAYERS, D_MODEL, M_IN_WIDTH), D_MODEL ** -0.5)
    a_b_gates = jnp.concatenate([nrm(ks[2], (N_A_LAYERS, M_HEADS), 0.1),
                                 3.0 + nrm(ks[3], (N_A_LAYERS, M_HEADS), 0.5)], axis=-1)
    a_conv_w = nrm(ks[4], (N_A_LAYERS, M_CONV, 2 * M_QK_WIDTH), M_CONV ** -0.5)
    a_conv_b = nrm(ks[5], (N_A_LAYERS, 2 * M_QK_WIDTH), 0.02)
    a_norm_w = 1.0 + nrm(ks[6], (N_A_LAYERS, M_V_WIDTH), 0.02)
    a_w_out = nrm(ks[7], (N_A_LAYERS, M_V_WIDTH, D_MODEL), DN_BETA * M_V_WIDTH ** -0.5)
    kv_w_down = nrm(ks[8], (D_MODEL, KV_RANK + ROPE_DIM), D_MODEL ** -0.5)
    kv_norm_w = 1.0 + nrm(ks[9], (KV_RANK,), 0.02)
    kv_w_up = nrm(ks[10], (KV_RANK, A_HEADS * (NOPE_DIM + V_DIM)), KV_RANK ** -0.5)
    b_w_dq = nrm(ks[11], (N_B_LAYERS, D_MODEL, Q_RANK), D_MODEL ** -0.5)
    b_q_norm_w = 1.0 + nrm(ks[12], (N_B_LAYERS, Q_RANK), 0.02)
    b_w_uq = nrm(ks[13], (N_B_LAYERS, Q_RANK, A_HEADS * (NOPE_DIM + ROPE_DIM)), Q_RANK ** -0.5)
    b_w_out = nrm(ks[14], (N_B_LAYERS, A_HEADS * V_DIM, D_MODEL), DN_BETA * (A_HEADS * V_DIM) ** -0.5)
    mlp_w1 = nrm(ks[15], (DEPTH, D_MODEL, D_FF), D_MODEL ** -0.5)
    mlp_w2 = nrm(ks[16], (DEPTH, D_FF, D_MODEL), DN_BETA * D_FF ** -0.5)
    ln1_g = 1.0 + nrm(ks[17], (DEPTH, D_MODEL), 0.02)
    ln1_b = nrm(ks[18], (DEPTH, D_MODEL), 0.02)
    ln2_g = 1.0 + nrm(ks[19], (DEPTH, D_MODEL), 0.02)
    ln2_b = nrm(ks[20], (DEPTH, D_MODEL), 0.02)
    return {"x": x, "a_w_in": a_w_in, "a_b_gates": a_b_gates, "a_conv_w": a_conv_w,
            "a_conv_b": a_conv_b, "a_norm_w": a_norm_w, "a_w_out": a_w_out,
            "kv_w_down": kv_w_down, "kv_norm_w": kv_norm_w, "kv_w_up": kv_w_up,
            "b_w_dq": b_w_dq, "b_q_norm_w": b_q_norm_w, "b_w_uq": b_w_uq, "b_w_out": b_w_out,
            "mlp_w1": mlp_w1, "mlp_w2": mlp_w2,
            "ln1_g": ln1_g, "ln1_b": ln1_b, "ln2_g": ln2_g, "ln2_b": ln2_b}


def reference(x, a_w_in, a_b_gates, a_conv_w, a_conv_b, a_norm_w, a_w_out,
              kv_w_down, kv_norm_w, kv_w_up, b_w_dq, b_q_norm_w, b_w_uq, b_w_out,
              mlp_w1, mlp_w2, ln1_g, ln1_b, ln2_g, ln2_b):
    S = x.shape[1]
    pos = jnp.arange(S, dtype=F32)
    inv_freq = ROPE_THETA ** (-jnp.arange(0, ROPE_DIM, 2, dtype=F32) / ROPE_DIM)
    ang = pos[:, None] * inv_freq[None, :]
    cos = jnp.cos(ang).astype(x.dtype)
    sin = jnp.sin(ang).astype(x.dtype)
    k_nope = k_rope = v_shared = None
    for layer in range(DEPTH):
        if layer < N_A_LAYERS:
            mix = mlstm_mixer(x, a_w_in[layer], a_b_gates[layer], a_conv_w[layer],
                              a_conv_b[layer], a_norm_w[layer], a_w_out[layer])
        else:
            if layer == N_A_LAYERS:
                k_nope, k_rope, v_shared = shared_latent_kv(x, kv_w_down, kv_norm_w, kv_w_up, cos, sin)
            j = layer - N_A_LAYERS
            mix = mla_mixer(x, k_nope, k_rope, v_shared, b_w_dq[j], b_q_norm_w[j],
                            b_w_uq[j], b_w_out[j], cos, sin)
        x = layer_norm(DN_ALPHA * x + mix, ln1_g[layer], ln1_b[layer])
        x = layer_norm(DN_ALPHA * x + squared_relu_mlp(x, mlp_w1[layer], mlp_w2[layer]),
                       ln2_g[layer], ln2_b[layer])
    return x
```

```python
import functools

import jax
import jax.numpy as jnp
from jax import lax
from jax.experimental import pallas as pl
from jax.experimental.pallas import tpu as pltpu

F32 = jnp.float32
BF16 = jnp.bfloat16

D_MODEL = 2048
DEPTH = 2
N_A_LAYERS = DEPTH // 2
M_HEADS = 8
M_QK_DIM = D_MODEL // (2 * M_HEADS)
M_V_DIM = D_MODEL // M_HEADS
M_CONV = 4
M_QK_WIDTH = M_HEADS * M_QK_DIM
M_V_WIDTH = M_HEADS * M_V_DIM
A_HEADS = 16
NOPE_DIM = 128
ROPE_DIM = 64
V_DIM = 128
KV_RANK = D_MODEL // 4
Q_RANK = D_MODEL // 4
ROPE_THETA = 10000.0
ATTN_CHUNK = 64
D_FF = 4 * D_MODEL
DN_ALPHA = (2 * DEPTH) ** 0.25
LN_EPS = 1e-5
RMS_EPS = 1e-6

LANES = 128
SUBLANES = 8
VMEM_LIMIT_BYTES = 56 * 1024 * 1024
QK_HEAD_PAD = 2 * LANES
MLSTM_CHUNK = 128


def _cparams(semantics):
    return pltpu.CompilerParams(dimension_semantics=semantics,
                                vmem_limit_bytes=VMEM_LIMIT_BYTES)


def _mm_kernel(x_ref, w_ref, o_ref, *, act):
    y = jnp.dot(x_ref[...], w_ref[...], preferred_element_type=F32)
    if act == "relu2":
        y = jnp.square(jnp.maximum(y, 0.0))
    o_ref[...] = y.astype(o_ref.dtype)


def _matmul(x, w, *, out_dtype, act=None, tm=1024, tn=1024, name):
    m, k = x.shape
    n = w.shape[1]
    tm, tn = min(tm, m), min(tn, n)
    assert m % tm == 0 and n % tn == 0
    return pl.pallas_call(
        functools.partial(_mm_kernel, act=act),
        grid=(m // tm, n // tn),
        in_specs=[pl.BlockSpec((tm, k), lambda i, j: (i, 0)),
                  pl.BlockSpec((k, tn), lambda i, j: (0, j))],
        out_specs=pl.BlockSpec((tm, tn), lambda i, j: (i, j)),
        out_shape=jax.ShapeDtypeStruct((m, n), out_dtype),
        compiler_params=_cparams(("parallel", "parallel")),
        name=name,
    )(x, w)


def _mm_ln_kernel(h_ref, w_ref, r_ref, g_ref, b_ref, of_ref, ob_ref, acc_ref, *, nk):
    kk = pl.program_id(1)

    @pl.when(kk == 0)
    def _():
        acc_ref[...] = DN_ALPHA * r_ref[...]

    acc_ref[...] += jnp.dot(h_ref[...], w_ref[...], preferred_element_type=F32)

    @pl.when(kk == nk - 1)
    def _():
        y = acc_ref[...]
        mu = jnp.mean(y, axis=-1, keepdims=True)
        yc = y - mu
        var = jnp.mean(yc * yc, axis=-1, keepdims=True)
        out = yc * lax.rsqrt(var + LN_EPS) * g_ref[...] + b_ref[...]
        of_ref[...] = out
        ob_ref[...] = out.astype(BF16)


def _matmul_deepnorm(h, w, resid, g, b, *, tm=512, tk=1024, name):
    m, k = h.shape
    n = w.shape[1]
    tm, tk = min(tm, m), min(tk, k)
    assert m % tm == 0 and k % tk == 0
    nk = k // tk
    return pl.pallas_call(
        functools.partial(_mm_ln_kernel, nk=nk),
        grid=(m // tm, nk),
        in_specs=[pl.BlockSpec((tm, tk), lambda i, kk: (i, kk)),
                  pl.BlockSpec((tk, n), lambda i, kk: (kk, 0)),
                  pl.BlockSpec((tm, n), lambda i, kk: (i, 0)),
                  pl.BlockSpec((1, n), lambda i, kk: (0, 0)),
                  pl.BlockSpec((1, n), lambda i, kk: (0, 0))],
        out_specs=[pl.BlockSpec((tm, n), lambda i, kk: (i, 0)),
                   pl.BlockSpec((tm, n), lambda i, kk: (i, 0))],
        out_shape=[jax.ShapeDtypeStruct((m, n), F32),
                   jax.ShapeDtypeStruct((m, n), BF16)],
        scratch_shapes=[pltpu.VMEM((tm, n), F32)],
        compiler_params=_cparams(("parallel", "arbitrary")),
        name=name,
    )(h, w, resid, g.reshape(1, n), b.reshape(1, n))


def _mlstm_kernel(qk_ref, v_ref, og_ref, g_ref, bg_ref, cw_ref, cb_ref, nw_ref, out_ref,
                  ext_ref, qa_ref, ka_ref, c_ref, n_ref, m_ref, *, ts, chunk):
    L = chunk
    halo = SUBLANES

    @pl.when(pl.program_id(1) == 0)
    def _():
        ext_ref[0:halo, :] = jnp.zeros((halo, 2 * M_QK_WIDTH), F32)
        c_ref[...] = jnp.zeros_like(c_ref)
        n_ref[...] = jnp.zeros_like(n_ref)
        m_ref[...] = jnp.zeros_like(m_ref)

    ext_ref[halo:halo + ts, :] = qk_ref[...].astype(F32)
    slab = 2 * LANES
    for cs in range(0, 2 * M_QK_WIDTH, slab):
        acc = jnp.broadcast_to(cb_ref[:, cs:cs + slab], (ts, slab))
        for j in range(M_CONV):
            off = halo - (M_CONV - 1) + j
            acc = acc + cw_ref[j:j + 1, cs:cs + slab] * ext_ref[off:off + ts, cs:cs + slab]
        y = acc * jax.nn.sigmoid(acc)
        if cs < M_QK_WIDTH:
            qa_ref[:, cs:cs + slab] = (y * (M_QK_DIM ** -0.5)).astype(BF16)
        else:
            ka_ref[:, cs - M_QK_WIDTH:cs - M_QK_WIDTH + slab] = y.astype(BF16)
    ext_ref[0:halo, :] = ext_ref[ts:ts + halo, :]

    row_i = lax.broadcasted_iota(jnp.int32, (L, L), 0)
    col_i = lax.broadcasted_iota(jnp.int32, (L, L), 1)
    causal = row_i >= col_i
    row_g = lax.broadcasted_iota(jnp.int32, (L, LANES), 0)
    lane_g = lax.broadcasted_iota(jnp.int32, (L, LANES), 1)

    def col_of(a, idx):
        return jnp.sum(jnp.where(lane_g == idx, a, 0.0), axis=1, keepdims=True)

    def chunk_step(c, carry):
        r0 = pl.multiple_of(c * L, L)
        gc = g_ref[pl.ds(r0, L), :] + bg_ref[...]
        lf = jnp.minimum(gc, 0.0) - jnp.log(1.0 + jnp.exp(-jnp.abs(gc)))
        bc = lf
        sh = 1
        while sh < L:
            bc = bc + jnp.where(row_g >= sh, pltpu.roll(bc, sh, axis=0), 0.0)
            sh *= 2
        g_t = gc.T
        b_t = bc.T
        for h in range(M_HEADS):
            qs = slice(h * M_QK_DIM, (h + 1) * M_QK_DIM)
            vs = slice(h * M_V_DIM, (h + 1) * M_V_DIM)
            q = qa_ref[pl.ds(r0, L), qs]
            k = ka_ref[pl.ds(r0, L), qs]
            v = v_ref[pl.ds(r0, L), vs]
            bcol = col_of(bc, M_HEADS + h)
            icol = col_of(gc, h)
            brow = b_t[M_HEADS + h:M_HEADS + h + 1, :]
            irow = g_t[h:h + 1, :]
            gtot = brow[:, L - 1:L]
            m_prev = m_ref[h:h + 1, 0:1]
            cmat = c_ref[h]
            nrow = n_ref[h:h + 1, :]

            dmat = jnp.where(causal, bcol - brow + irow, -jnp.inf)
            inter = bcol + m_prev
            m_t = jnp.maximum(jnp.max(dmat, axis=1, keepdims=True), inter)
            w_intra = jnp.exp(dmat - m_t)
            w_inter = jnp.exp(inter - m_t)
            s = lax.dot_general(q, k, (((1,), (1,)), ((), ())), preferred_element_type=F32)
            sw = s * w_intra
            qc = jnp.dot(q, cmat.astype(BF16), preferred_element_type=F32)
            num = jnp.dot(sw.astype(BF16), v, preferred_element_type=F32) + w_inter * qc
            qn = jnp.sum(q.astype(F32) * nrow, axis=1, keepdims=True)
            den = jnp.sum(sw, axis=1, keepdims=True) + w_inter * qn
            hh = num * (1.0 / jnp.maximum(jnp.abs(den), jnp.exp(-m_t)))

            a_row = gtot - brow + irow
            a_col = gtot - bcol + icol
            m_new = jnp.maximum(gtot + m_prev, jnp.max(a_row, axis=1, keepdims=True))
            decay = jnp.exp(gtot + m_prev - m_new)
            kw = k.astype(F32) * jnp.exp(a_col - m_new)
            c_ref[h] = decay * cmat + jnp.dot(kw.T.astype(BF16), v, preferred_element_type=F32)
            n_ref[h:h + 1, :] = decay * nrow + jnp.sum(kw, axis=0, keepdims=True)
            m_ref[h:h + 1, :] = jnp.broadcast_to(m_new, (1, LANES))

            mu = jnp.mean(hh, axis=1, keepdims=True)
            hc = hh - mu
            var = jnp.mean(hc * hc, axis=1, keepdims=True)
            hn = hc * lax.rsqrt(var + LN_EPS) * nw_ref[:, vs]
            og = og_ref[pl.ds(r0, L), vs].astype(F32)
            out_ref[pl.ds(r0, L), vs] = (jax.nn.sigmoid(og) * hn).astype(BF16)
        return carry

    lax.fori_loop(0, ts // L, chunk_step, 0)


def _mlstm_scan(proj, gates, b_gates, conv_w, conv_b, norm_w, *, batch, seq, ts=512):
    m = proj.shape[0]
    ts = min(ts, seq)
    chunk = min(MLSTM_CHUNK, ts)
    assert seq % ts == 0 and ts % chunk == 0
    ns = seq // ts
    qkw = 2 * M_QK_WIDTH
    assert qkw == M_V_WIDTH
    row = lambda b, s: b * ns + s
    bg = jnp.zeros((1, LANES), F32).at[0, :2 * M_HEADS].set(b_gates)
    return pl.pallas_call(
        functools.partial(_mlstm_kernel, ts=ts, chunk=chunk),
        grid=(batch, ns),
        in_specs=[pl.BlockSpec((ts, qkw), lambda b, s: (row(b, s), 0)),
                  pl.BlockSpec((ts, M_V_WIDTH), lambda b, s: (row(b, s), 1)),
                  pl.BlockSpec((ts, M_V_WIDTH), lambda b, s: (row(b, s), 2)),
                  pl.BlockSpec((ts, LANES), lambda b, s: (row(b, s), 0)),
                  pl.BlockSpec((1, LANES), lambda b, s: (0, 0)),
                  pl.BlockSpec((M_CONV, qkw), lambda b, s: (0, 0)),
                  pl.BlockSpec((1, qkw), lambda b, s: (0, 0)),
                  pl.BlockSpec((1, M_V_WIDTH), lambda b, s: (0, 0))],
        out_specs=pl.BlockSpec((ts, M_V_WIDTH), lambda b, s: (row(b, s), 0)),
        out_shape=jax.ShapeDtypeStruct((m, M_V_WIDTH), BF16),
        scratch_shapes=[pltpu.VMEM((ts + SUBLANES, qkw), F32),
                        pltpu.VMEM((ts, M_QK_WIDTH), BF16),
                        pltpu.VMEM((ts, M_QK_WIDTH), BF16),
                        pltpu.VMEM((M_HEADS, M_QK_DIM, M_V_DIM), F32),
                        pltpu.VMEM((M_HEADS, M_QK_DIM), F32),
                        pltpu.VMEM((M_HEADS, LANES), F32)],
        compiler_params=_cparams(("parallel", "arbitrary")),
        name="mlstm_scan",
    )(proj, proj, proj, gates, bg, conv_w, conv_b.reshape(1, qkw), norm_w.reshape(1, M_V_WIDTH))


def _rope_lanes(r, cos_t, sin_t):
    half = ROPE_DIM // 2
    lane = lax.broadcasted_iota(jnp.int32, r.shape, 1)
    rot = jnp.where(lane < half, -pltpu.roll(r, LANES - half, axis=1), pltpu.roll(r, half, axis=1))
    return r * cos_t + rot * sin_t


def _rms(y, g):
    ms = jnp.mean(y * y, axis=-1, keepdims=True)
    return y * lax.rsqrt(ms + RMS_EPS) * g


def _kv_kernel(x_ref, wdc_ref, wdr_ref, nw_ref, wup_ref, cos_ref, sin_ref, k_ref, v_ref):
    x = x_ref[...]
    c = _rms(jnp.dot(x, wdc_ref[...], preferred_element_type=F32), nw_ref[...]).astype(BF16)
    kr = jnp.dot(x, wdr_ref[...], preferred_element_type=F32)
    kr = _rope_lanes(kr, cos_ref[...], sin_ref[...]).astype(BF16)
    per_head = NOPE_DIM + V_DIM
    for h in range(A_HEADS):
        kv = jnp.dot(c, wup_ref[:, h * per_head:(h + 1) * per_head], preferred_element_type=F32)
        k_ref[:, h * QK_HEAD_PAD:h * QK_HEAD_PAD + NOPE_DIM] = kv[:, :NOPE_DIM].astype(BF16)
        k_ref[:, h * QK_HEAD_PAD + NOPE_DIM:(h + 1) * QK_HEAD_PAD] = kr
        v_ref[:, h * V_DIM:(h + 1) * V_DIM] = kv[:, NOPE_DIM:].astype(BF16)


def _shared_kv(xb, wd_c, wd_r, norm_w, w_up, cos_t, sin_t, *, seq, tm=512):
    m, d = xb.shape
    tm = min(tm, seq)
    assert seq % tm == 0
    npos = seq // tm
    full = lambda a: pl.BlockSpec(a.shape, lambda i: (0, 0))
    nw = norm_w.reshape(1, KV_RANK)
    return pl.pallas_call(
        _kv_kernel,
        grid=(m // tm,),
        in_specs=[pl.BlockSpec((tm, d), lambda i: (i, 0)),
                  full(wd_c), full(wd_r), full(nw), full(w_up),
                  pl.BlockSpec((tm, LANES), lambda i: (i % npos, 0)),
                  pl.BlockSpec((tm, LANES), lambda i: (i % npos, 0))],
        out_specs=[pl.BlockSpec((tm, A_HEADS * QK_HEAD_PAD), lambda i: (i, 0)),
                   pl.BlockSpec((tm, A_HEADS * V_DIM), lambda i: (i, 0))],
        out_shape=[jax.ShapeDtypeStruct((m, A_HEADS * QK_HEAD_PAD), BF16),
                   jax.ShapeDtypeStruct((m, A_HEADS * V_DIM), BF16)],
        compiler_params=_cparams(("parallel",)),
        name="mla_shared_kv",
    )(xb, wd_c, wd_r, nw, w_up, cos_t, sin_t)


def _q_kernel(x_ref, wdq_ref, nw_ref, wuq_ref, cos_ref, sin_ref, q_ref):
    cq = _rms(jnp.dot(x_ref[...], wdq_ref[...], preferred_element_type=F32), nw_ref[...]).astype(BF16)
    cos_t = cos_ref[...]
    sin_t = sin_ref[...]
    for h in range(A_HEADS):
        qh = jnp.dot(cq, wuq_ref[:, h * QK_HEAD_PAD:(h + 1) * QK_HEAD_PAD], preferred_element_type=F32)
        q_ref[:, h * QK_HEAD_PAD:h * QK_HEAD_PAD + NOPE_DIM] = qh[:, :NOPE_DIM].astype(BF16)
        q_ref[:, h * QK_HEAD_PAD + NOPE_DIM:(h + 1) * QK_HEAD_PAD] = (
            _rope_lanes(qh[:, NOPE_DIM:], cos_t, sin_t).astype(BF16))


def _mla_q(xb, w_dq, norm_w, w_uq_pad, cos_t, sin_t, *, seq, tm=512):
    m, d = xb.shape
    tm = min(tm, seq)
    assert seq % tm == 0
    npos = seq // tm
    full = lambda a: pl.BlockSpec(a.shape, lambda i: (0, 0))
    nw = norm_w.reshape(1, Q_RANK)
    return pl.pallas_call(
        _q_kernel,
        grid=(m // tm,),
        in_specs=[pl.BlockSpec((tm, d), lambda i: (i, 0)),
                  full(w_dq), full(nw), full(w_uq_pad),
                  pl.BlockSpec((tm, LANES), lambda i: (i % npos, 0)),
                  pl.BlockSpec((tm, LANES), lambda i: (i % npos, 0))],
        out_specs=pl.BlockSpec((tm, A_HEADS * QK_HEAD_PAD), lambda i: (i, 0)),
        out_shape=jax.ShapeDtypeStruct((m, A_HEADS * QK_HEAD_PAD), BF16),
        compiler_params=_cparams(("parallel",)),
        name="mla_q",
    )(xb, w_dq, nw, w_uq_pad, cos_t, sin_t)


def _attn_kernel(q_ref, k_ref, v_ref, o_ref, *, seq, tq):
    scale = (NOPE_DIM + ROPE_DIM) ** -0.5
    nt = (((1,), (1,)), ((), ()))
    qc = lax.broadcasted_iota(jnp.int32, (tq, tq), 0) // ATTN_CHUNK
    kc = lax.broadcasted_iota(jnp.int32, (tq, tq), 1) // ATTN_CHUNK
    visible = qc >= kc
    for i in range(seq // tq):
        lo = i * tq
        q = q_ref[lo:lo + tq, :]
        sd = lax.dot_general(q, k_ref[lo:lo + tq, :], nt, preferred_element_type=F32) * scale
        sd = jnp.where(visible, sd, -jnp.inf)
        mx = jnp.max(sd, axis=1, keepdims=True)
        if i > 0:
            so = lax.dot_general(q, k_ref[0:lo, :], nt, preferred_element_type=F32) * scale
            mx = jnp.maximum(mx, jnp.max(so, axis=1, keepdims=True))
            po = jnp.exp(so - mx)
        pd = jnp.exp(sd - mx)
        den = jnp.sum(pd, axis=1, keepdims=True)
        acc = jnp.dot(pd.astype(BF16), v_ref[lo:lo + tq, :], preferred_element_type=F32)
        if i > 0:
            den = den + jnp.sum(po, axis=1, keepdims=True)
            acc = acc + jnp.dot(po.astype(BF16), v_ref[0:lo, :], preferred_element_type=F32)
        o_ref[lo:lo + tq, :] = (acc * (1.0 / den)).astype(BF16)


def _attention(q_cat, k_cat, v, *, batch, seq, tq=256):
    m = q_cat.shape[0]
    tq = min(tq, seq)
    assert seq % tq == 0 and tq % ATTN_CHUNK == 0
    return pl.pallas_call(
        functools.partial(_attn_kernel, seq=seq, tq=tq),
        grid=(batch, A_HEADS),
        in_specs=[pl.BlockSpec((seq, QK_HEAD_PAD), lambda b, h: (b, h)),
                  pl.BlockSpec((seq, QK_HEAD_PAD), lambda b, h: (b, h)),
                  pl.BlockSpec((seq, V_DIM), lambda b, h: (b, h))],
        out_specs=pl.BlockSpec((seq, V_DIM), lambda b, h: (b, h)),
        out_shape=jax.ShapeDtypeStruct((m, A_HEADS * V_DIM), BF16),
        compiler_params=_cparams(("parallel", "parallel")),
        name="mla_attention",
    )(q_cat, k_cat, v)


def _rope_tables(seq):
    pos = jnp.arange(seq, dtype=F32)
    inv_freq = ROPE_THETA ** (-jnp.arange(0, ROPE_DIM, 2, dtype=F32) / ROPE_DIM)
    ang = pos[:, None] * inv_freq[None, :]
    pad = jnp.zeros((seq, LANES - ROPE_DIM), F32)
    cos, sin = jnp.cos(ang), jnp.sin(ang)
    return (jnp.concatenate([cos, cos, pad], axis=1), jnp.concatenate([sin, sin, pad], axis=1))


def kernel(x, a_w_in, a_b_gates, a_conv_w, a_conv_b, a_norm_w, a_w_out, kv_w_down, kv_norm_w,
           kv_w_up, b_w_dq, b_q_norm_w, b_w_uq, b_w_out, mlp_w1, mlp_w2, ln1_g, ln1_b, ln2_g, ln2_b):
    batch, seq, d = x.shape
    m = batch * seq
    xf = x.reshape(m, d)
    xb = xf.astype(BF16)
    cos_t, sin_t = _rope_tables(seq)
    kv = None
    proj_w = 2 * M_QK_WIDTH + 2 * M_V_WIDTH
    for layer in range(DEPTH):
        if layer < N_A_LAYERS:
            w_in = a_w_in[layer]
            w_main = w_in[:, :proj_w].astype(BF16)
            w_gate = jnp.pad(w_in[:, proj_w:], ((0, 0), (0, LANES - 2 * M_HEADS))).astype(BF16)
            proj = _matmul(xb, w_main, out_dtype=BF16, name=f"mlstm_in_proj_{layer}")
            gates = _matmul(xb, w_gate, out_dtype=F32, name=f"mlstm_gate_proj_{layer}")
            mix_in = _mlstm_scan(proj, gates, a_b_gates[layer], a_conv_w[layer], a_conv_b[layer],
                                 a_norm_w[layer], batch=batch, seq=seq)
            w_o = a_w_out[layer].astype(BF16)
        else:
            j = layer - N_A_LAYERS
            if kv is None:
                wd = kv_w_down.astype(BF16)
                wd_r = jnp.pad(wd[:, KV_RANK:], ((0, 0), (0, LANES - ROPE_DIM)))
                kv = _shared_kv(xb, wd[:, :KV_RANK], wd_r, kv_norm_w, kv_w_up.astype(BF16),
                                cos_t, sin_t, seq=seq)
            w_uq = b_w_uq[j].reshape(Q_RANK, A_HEADS, NOPE_DIM + ROPE_DIM)
            w_uq = jnp.pad(w_uq, ((0, 0), (0, 0), (0, QK_HEAD_PAD - NOPE_DIM - ROPE_DIM)))
            w_uq = w_uq.reshape(Q_RANK, A_HEADS * QK_HEAD_PAD).astype(BF16)
            q_cat = _mla_q(xb, b_w_dq[j].astype(BF16), b_q_norm_w[j], w_uq, cos_t, sin_t, seq=seq)
            mix_in = _attention(q_cat, kv[0], kv[1], batch=batch, seq=seq)
            w_o = b_w_out[j].astype(BF16)
        xf, xb = _matmul_deepnorm(mix_in, w_o, xf, ln1_g[layer], ln1_b[layer],
                                  name=f"mix_out_deepnorm_{layer}")
        hid = _matmul(xb, mlp_w1[layer].astype(BF16), out_dtype=BF16, act="relu2",
                      name=f"mlp_up_{layer}")
        xf, xb = _matmul_deepnorm(hid, mlp_w2[layer].astype(BF16), xf, ln2_g[layer], ln2_b[layer],
                                  name=f"mlp_down_deepnorm_{layer}")
    return xf.reshape(batch, seq, d)
```

```python
import functools
import math

import jax
import jax.numpy as jnp
from jax import lax
from jax.experimental import pallas as pl
from jax.experimental.pallas import tpu as pltpu

F32 = jnp.float32
BF16 = jnp.bfloat16

D_MODEL = 2048
DEPTH = 2
N_A_LAYERS = DEPTH // 2
M_HEADS = 8
M_QK_DIM = D_MODEL // (2 * M_HEADS)
M_V_DIM = D_MODEL // M_HEADS
M_CONV = 4
M_QK_WIDTH = M_HEADS * M_QK_DIM
M_V_WIDTH = M_HEADS * M_V_DIM
M_PROJ_WIDTH = 2 * M_QK_WIDTH + 2 * M_V_WIDTH
A_HEADS = 16
NOPE_DIM = 128
ROPE_DIM = 64
V_DIM = 128
KV_RANK = D_MODEL // 4
Q_RANK = D_MODEL // 4
ROPE_THETA = 10000.0
ATTN_CHUNK = 64
D_FF = 4 * D_MODEL
DN_ALPHA = (2 * DEPTH) ** 0.25
LN_EPS = 1e-5
RMS_EPS = 1e-6

LANES = 128
SUBLANES = 8
VMEM_LIMIT_BYTES = 56 * 1024 * 1024
QK_HEAD_PAD = 2 * LANES
V_HEAD_PAD = 2 * LANES
MLSTM_CHUNK = 128
M_STATE_WIDTH = M_V_DIM + LANES
EPILOGUE_ROWS = 128


def _cparams(semantics):
    return pltpu.CompilerParams(dimension_semantics=semantics,
                                vmem_limit_bytes=VMEM_LIMIT_BYTES)


def _in_proj_kernel(x_ref, w_ref, wg_ref, o_ref, g_ref, xb_ref):
    @pl.when(pl.program_id(1) == 0)
    def _():
        xb_ref[...] = x_ref[...].astype(BF16)
        g_ref[...] = jnp.dot(xb_ref[...], wg_ref[...], preferred_element_type=F32)

    o_ref[...] = jnp.dot(xb_ref[...], w_ref[...].astype(BF16),
                         preferred_element_type=F32).astype(o_ref.dtype)


def _in_proj(xf, w_in_all, layer, w_gate, *, tm=1024, tn=1024):
    m, k = xf.shape
    tm = min(tm, m)
    assert m % tm == 0 and M_PROJ_WIDTH % tn == 0
    ng = w_gate.shape[1]
    return pl.pallas_call(
        _in_proj_kernel,
        grid=(m // tm, M_PROJ_WIDTH // tn),
        in_specs=[pl.BlockSpec((tm, k), lambda i, j: (i, 0)),
                  pl.BlockSpec((None, k, tn), lambda i, j: (layer, 0, j)),
                  pl.BlockSpec((k, ng), lambda i, j: (0, 0))],
        out_specs=[pl.BlockSpec((tm, tn), lambda i, j: (i, j)),
                   pl.BlockSpec((tm, ng), lambda i, j: (i, 0))],
        out_shape=[jax.ShapeDtypeStruct((m, M_PROJ_WIDTH), BF16),
                   jax.ShapeDtypeStruct((m, ng), F32)],
        scratch_shapes=[pltpu.VMEM((tm, k), BF16)],
        compiler_params=_cparams(("parallel", "arbitrary")),
        name="mlstm_in_proj",
    )(xf, w_in_all, w_gate)


def _mlp_up_kernel(x_ref, w_ref, o_ref):
    y = jnp.dot(x_ref[...], w_ref[...].astype(BF16), preferred_element_type=F32)
    o_ref[...] = jnp.square(jnp.maximum(y, 0.0)).astype(o_ref.dtype)


def _mlp_up(xb, w1_all, layer, *, tm=1024, tn=1024, name):
    m, k = xb.shape
    n = w1_all.shape[2]
    tm = min(tm, m)
    assert m % tm == 0 and n % tn == 0
    return pl.pallas_call(
        _mlp_up_kernel,
        grid=(m // tm, n // tn),
        in_specs=[pl.BlockSpec((tm, k), lambda i, j: (i, 0)),
                  pl.BlockSpec((None, k, tn), lambda i, j: (layer, 0, j))],
        out_specs=pl.BlockSpec((tm, tn), lambda i, j: (i, j)),
        out_shape=jax.ShapeDtypeStruct((m, n), BF16),
        compiler_params=_cparams(("parallel", "parallel")),
        name=name,
    )(xb, w1_all)


def _mm_ln_kernel(h_ref, w_ref, r_ref, g_ref, b_ref, of_ref, ob_ref, *, nk, tm):
    kk = pl.program_id(1)

    @pl.when(kk == 0)
    def _():
        of_ref[...] = DN_ALPHA * r_ref[...]

    of_ref[...] += jnp.dot(h_ref[...], w_ref[...], preferred_element_type=F32)

    @pl.when(kk == nk - 1)
    def _():
        rows = min(EPILOGUE_ROWS, tm)

        def slab(s, carry):
            r0 = pl.multiple_of(s * rows, rows)
            y = of_ref[pl.ds(r0, rows), :]
            mu = jnp.mean(y, axis=-1, keepdims=True)
            yc = y - mu
            var = jnp.mean(yc * yc, axis=-1, keepdims=True)
            out = yc * lax.rsqrt(var + LN_EPS) * g_ref[...] + b_ref[...]
            of_ref[pl.ds(r0, rows), :] = out
            ob_ref[pl.ds(r0, rows), :] = out.astype(BF16)
            return carry

        lax.fori_loop(0, tm // rows, slab, 0)


def _matmul_deepnorm(h, w, resid, g, b, *, tm=1024, tk=512, name):
    m, k = h.shape
    n = w.shape[1]
    tm, tk = min(tm, m), min(tk, k)
    assert m % tm == 0 and k % tk == 0
    nk = k // tk
    return pl.pallas_call(
        functools.partial(_mm_ln_kernel, nk=nk, tm=tm),
        grid=(m // tm, nk),
        in_specs=[pl.BlockSpec((tm, tk), lambda i, kk: (i, kk)),
                  pl.BlockSpec((tk, n), lambda i, kk: (kk, 0)),
                  pl.BlockSpec((tm, n), lambda i, kk: (i, 0)),
                  pl.BlockSpec((1, n), lambda i, kk: (0, 0)),
                  pl.BlockSpec((1, n), lambda i, kk: (0, 0))],
        out_specs=[pl.BlockSpec((tm, n), lambda i, kk: (i, 0)),
                   pl.BlockSpec((tm, n), lambda i, kk: (i, 0))],
        out_shape=[jax.ShapeDtypeStruct((m, n), F32),
                   jax.ShapeDtypeStruct((m, n), BF16)],
        compiler_params=_cparams(("parallel", "arbitrary")),
        name=name,
    )(h, w, resid, g.reshape(1, n), b.reshape(1, n))


def _mlstm_kernel(qk_ref, v_ref, og_ref, g_ref, bg_ref, cw_ref, cb_ref, nw_ref, out_ref,
                  ext_ref, qa_ref, ka_ref, c_ref, m_ref, *, ts, chunk):
    L = chunk
    halo = SUBLANES

    @pl.when(pl.program_id(1) == 0)
    def _():
        ext_ref[0:halo, :] = jnp.zeros((halo, 2 * M_QK_WIDTH), F32)
        c_ref[...] = jnp.zeros_like(c_ref)
        m_ref[...] = jnp.zeros_like(m_ref)

    ext_ref[halo:halo + ts, :] = qk_ref[...].astype(F32)
    slab = 2 * LANES
    for cs in range(0, 2 * M_QK_WIDTH, slab):
        acc = jnp.broadcast_to(cb_ref[:, cs:cs + slab], (ts, slab))
        for j in range(M_CONV):
            off = halo - (M_CONV - 1) + j
            acc = acc + cw_ref[j:j + 1, cs:cs + slab] * ext_ref[off:off + ts, cs:cs + slab]
        y = acc * jax.nn.sigmoid(acc)
        if cs < M_QK_WIDTH:
            qa_ref[:, cs:cs + slab] = (y * (M_QK_DIM ** -0.5)).astype(BF16)
        else:
            ka_ref[:, cs - M_QK_WIDTH:cs - M_QK_WIDTH + slab] = y.astype(BF16)
    ext_ref[0:halo, :] = ext_ref[ts:ts + halo, :]

    causal = (lax.broadcasted_iota(jnp.int32, (L, L), 0)
              >= lax.broadcasted_iota(jnp.int32, (L, L), 1))
    row_g = lax.broadcasted_iota(jnp.int32, (L, LANES), 0)
    ones_cols = jnp.ones((L, LANES), BF16)

    def scan_rows(a, op, fill):
        sh = 1
        while sh < L:
            a = op(a, jnp.where(row_g >= sh, pltpu.roll(a, sh, axis=0), fill))
            sh *= 2
        return a

    def lane_bcast(a, h):
        return jnp.broadcast_to(a[:, h:h + 1], (a.shape[0], LANES))

    def chunk_step(c, carry):
        r0 = pl.multiple_of(c * L, L)
        gi = g_ref[pl.ds(r0, L), 0:LANES] + bg_ref[:, 0:LANES]
        gf = g_ref[pl.ds(r0, L), LANES:2 * LANES] + bg_ref[:, LANES:2 * LANES]
        lf = jnp.minimum(gf, 0.0) - jnp.log(1.0 + jnp.exp(-jnp.abs(gf)))
        bc = scan_rows(lf, jnp.add, 0.0)
        u = gi - bc
        cm = scan_rows(u, jnp.maximum, -jnp.inf)
        m_prev = m_ref[0:1, :]
        big_m = jnp.maximum(cm, m_prev)
        neg_big_m = -big_m
        w_inter = jnp.exp(m_prev - big_m)
        exp_neg_mt = jnp.exp(-(bc + big_m))
        gtot = bc[L - 1:L, :]
        m_new = gtot + jnp.maximum(m_prev, cm[L - 1:L, :])
        decay = jnp.exp(gtot + m_prev - m_new)
        wk = jnp.exp(gtot + u - m_new)
        m_ref[...] = jnp.broadcast_to(m_new, m_ref.shape)
        u_t = u.T
        wk_t = wk.T

        for h in range(M_HEADS):
            qs = slice(h * M_QK_DIM, (h + 1) * M_QK_DIM)
            vs = slice(h * M_V_DIM, (h + 1) * M_V_DIM)
            q = qa_ref[pl.ds(r0, L), qs]
            k = ka_ref[pl.ds(r0, L), qs]
            vext = jnp.concatenate([v_ref[pl.ds(r0, L), vs], ones_cols], axis=1)
            cext = c_ref[h]

            w_intra = jnp.exp(jnp.where(causal, u_t[h:h + 1, :] + lane_bcast(neg_big_m, h), -jnp.inf))
            s = lax.dot_general(q, k, (((1,), (1,)), ((), ())), preferred_element_type=F32)
            sw = (s * w_intra).astype(BF16)
            wib = lane_bcast(w_inter, h)
            inter = jnp.dot(q, cext.astype(BF16), preferred_element_type=F32)
            ne = (jnp.dot(sw, vext, preferred_element_type=F32)
                  + jnp.concatenate([wib] * (M_STATE_WIDTH // LANES), axis=1) * inter)
            den = ne[:, M_V_DIM:]
            rcp = 1.0 / jnp.maximum(jnp.abs(den), lane_bcast(exp_neg_mt, h))
            hh = ne[:, :M_V_DIM] * jnp.concatenate([rcp] * (M_V_DIM // LANES), axis=1)

            kw_t = (k.astype(F32).T * wk_t[h:h + 1, :]).astype(BF16)
            dec = jnp.broadcast_to(decay[:, h:h + 1], cext.shape)
            c_ref[h] = dec * cext + jnp.dot(kw_t, vext, preferred_element_type=F32)

            mu = jnp.mean(hh, axis=1, keepdims=True)
            hc = hh - mu
            var = jnp.mean(hc * hc, axis=1, keepdims=True)
            hn = hc * lax.rsqrt(var + LN_EPS) * nw_ref[:, vs]
            og = og_ref[pl.ds(r0, L), vs].astype(F32)
            out_ref[pl.ds(r0, L), vs] = (jax.nn.sigmoid(og) * hn).astype(BF16)
        return carry

    lax.fori_loop(0, ts // L, chunk_step, 0)


def _mlstm_scan(proj, gates, b_gates, conv_w, conv_b, norm_w, *, batch, seq, ts=512):
    m = proj.shape[0]
    ts = min(ts, seq)
    chunk = min(MLSTM_CHUNK, ts)
    assert seq % ts == 0 and ts % chunk == 0
    ns = seq // ts
    qkw = 2 * M_QK_WIDTH
    assert qkw == M_V_WIDTH
    row = lambda b, s: b * ns + s
    bg = jnp.zeros((1, 2 * LANES), F32)
    bg = bg.at[0, :M_HEADS].set(b_gates[:M_HEADS]).at[0, LANES:LANES + M_HEADS].set(b_gates[M_HEADS:])
    return pl.pallas_call(
        functools.partial(_mlstm_kernel, ts=ts, chunk=chunk),
        grid=(batch, ns),
        in_specs=[pl.BlockSpec((ts, qkw), lambda b, s: (row(b, s), 0)),
                  pl.BlockSpec((ts, M_V_WIDTH), lambda b, s: (row(b, s), 1)),
                  pl.BlockSpec((ts, M_V_WIDTH), lambda b, s: (row(b, s), 2)),
                  pl.BlockSpec((ts, 2 * LANES), lambda b, s: (row(b, s), 0)),
                  pl.BlockSpec((1, 2 * LANES), lambda b, s: (0, 0)),
                  pl.BlockSpec((M_CONV, qkw), lambda b, s: (0, 0)),
                  pl.BlockSpec((1, qkw), lambda b, s: (0, 0)),
                  pl.BlockSpec((1, M_V_WIDTH), lambda b, s: (0, 0))],
        out_specs=pl.BlockSpec((ts, M_V_WIDTH), lambda b, s: (row(b, s), 0)),
        out_shape=jax.ShapeDtypeStruct((m, M_V_WIDTH), BF16),
        scratch_shapes=[pltpu.VMEM((ts + SUBLANES, qkw), F32),
                        pltpu.VMEM((ts, M_QK_WIDTH), BF16),
                        pltpu.VMEM((ts, M_QK_WIDTH), BF16),
                        pltpu.VMEM((M_HEADS, M_QK_DIM, M_STATE_WIDTH), F32),
                        pltpu.VMEM((SUBLANES, LANES), F32)],
        compiler_params=_cparams(("parallel", "arbitrary")),
        name="mlstm_scan",
    )(proj, proj, proj, gates, bg, conv_w, conv_b.reshape(1, qkw), norm_w.reshape(1, M_V_WIDTH))


def _rope_lanes(r, cos_t, sin_t):
    half = ROPE_DIM // 2
    lane = lax.broadcasted_iota(jnp.int32, r.shape, 1)
    rot = jnp.where(lane < half, -pltpu.roll(r, LANES - half, axis=1), pltpu.roll(r, half, axis=1))
    return r * cos_t + rot * sin_t


def _rms(y, g):
    ms = jnp.mean(y * y, axis=-1, keepdims=True)
    return y * lax.rsqrt(ms + RMS_EPS) * g


def _kv_kernel(x_ref, wdc_ref, wdr_ref, nw_ref, wup_ref, cos_ref, sin_ref, k_ref, v_ref):
    x = x_ref[...]
    c = _rms(jnp.dot(x, wdc_ref[...], preferred_element_type=F32), nw_ref[...]).astype(BF16)
    kr = jnp.dot(x, wdr_ref[...], preferred_element_type=F32)
    kr = _rope_lanes(kr, cos_ref[...], sin_ref[...]).astype(BF16)
    ones_cols = jnp.ones((x.shape[0], V_HEAD_PAD - V_DIM), BF16)
    per_head = NOPE_DIM + V_DIM
    for h in range(A_HEADS):
        kv = jnp.dot(c, wup_ref[:, h * per_head:(h + 1) * per_head], preferred_element_type=F32)
        k_ref[:, h * QK_HEAD_PAD:h * QK_HEAD_PAD + NOPE_DIM] = kv[:, :NOPE_DIM].astype(BF16)
        k_ref[:, h * QK_HEAD_PAD + NOPE_DIM:(h + 1) * QK_HEAD_PAD] = kr
        v_ref[:, h * V_HEAD_PAD:h * V_HEAD_PAD + V_DIM] = kv[:, NOPE_DIM:].astype(BF16)
        v_ref[:, h * V_HEAD_PAD + V_DIM:(h + 1) * V_HEAD_PAD] = ones_cols


def _shared_kv(xb, wd_c, wd_r, norm_w, w_up, cos_t, sin_t, *, seq, tm=512):
    m, d = xb.shape
    tm = min(tm, seq)
    assert seq % tm == 0
    npos = seq // tm
    full = lambda a: pl.BlockSpec(a.shape, lambda i: (0, 0))
    nw = norm_w.reshape(1, KV_RANK)
    return pl.pallas_call(
        _kv_kernel,
        grid=(m // tm,),
        in_specs=[pl.BlockSpec((tm, d), lambda i: (i, 0)),
                  full(wd_c), full(wd_r), full(nw), full(w_up),
                  pl.BlockSpec((tm, LANES), lambda i: (i % npos, 0)),
                  pl.BlockSpec((tm, LANES), lambda i: (i % npos, 0))],
        out_specs=[pl.BlockSpec((tm, A_HEADS * QK_HEAD_PAD), lambda i: (i, 0)),
                   pl.BlockSpec((tm, A_HEADS * V_HEAD_PAD), lambda i: (i, 0))],
        out_shape=[jax.ShapeDtypeStruct((m, A_HEADS * QK_HEAD_PAD), BF16),
                   jax.ShapeDtypeStruct((m, A_HEADS * V_HEAD_PAD), BF16)],
        compiler_params=_cparams(("parallel",)),
        name="mla_shared_kv",
    )(xb, wd_c, wd_r, nw, w_up, cos_t, sin_t)


def _q_kernel(x_ref, wdq_ref, nw_ref, wuq_ref, cos_ref, sin_ref, q_ref):
    qscale = (NOPE_DIM + ROPE_DIM) ** -0.5 * math.log2(math.e)
    cq = _rms(jnp.dot(x_ref[...], wdq_ref[...], preferred_element_type=F32), nw_ref[...]).astype(BF16)
    cos_t = cos_ref[...]
    sin_t = sin_ref[...]
    for h in range(A_HEADS):
        qh = jnp.dot(cq, wuq_ref[:, h * QK_HEAD_PAD:(h + 1) * QK_HEAD_PAD], preferred_element_type=F32)
        q_ref[:, h * QK_HEAD_PAD:h * QK_HEAD_PAD + NOPE_DIM] = (qh[:, :NOPE_DIM] * qscale).astype(BF16)
        q_ref[:, h * QK_HEAD_PAD + NOPE_DIM:(h + 1) * QK_HEAD_PAD] = (
            _rope_lanes(qh[:, NOPE_DIM:], cos_t, sin_t) * qscale).astype(BF16)


def _mla_q(xb, w_dq, norm_w, w_uq_pad, cos_t, sin_t, *, seq, tm=512):
    m, d = xb.shape
    tm = min(tm, seq)
    assert seq % tm == 0
    npos = seq // tm
    full = lambda a: pl.BlockSpec(a.shape, lambda i: (0, 0))
    nw = norm_w.reshape(1, Q_RANK)
    return pl.pallas_call(
        _q_kernel,
        grid=(m // tm,),
        in_specs=[pl.BlockSpec((tm, d), lambda i: (i, 0)),
                  full(w_dq), full(nw), full(w_uq_pad),
                  pl.BlockSpec((tm, LANES), lambda i: (i % npos, 0)),
                  pl.BlockSpec((tm, LANES), lambda i: (i % npos, 0))],
        out_specs=pl.BlockSpec((tm, A_HEADS * QK_HEAD_PAD), lambda i: (i, 0)),
        out_shape=jax.ShapeDtypeStruct((m, A_HEADS * QK_HEAD_PAD), BF16),
        compiler_params=_cparams(("parallel",)),
        name="mla_q",
    )(xb, w_dq, nw, w_uq_pad, cos_t, sin_t)


def _attn_kernel(q_ref, k_ref, v_ref, o_ref, *, seq, tq):
    nt = (((1,), (1,)), ((), ()))
    qc = lax.broadcasted_iota(jnp.int32, (tq, tq), 0) // ATTN_CHUNK
    kc = lax.broadcasted_iota(jnp.int32, (tq, tq), 1) // ATTN_CHUNK
    visible = qc >= kc
    for i in range(seq // tq):
        lo, hi = i * tq, (i + 1) * tq
        s = lax.dot_general(q_ref[lo:hi, :], k_ref[0:hi, :], nt, preferred_element_type=F32)
        sd = jnp.where(visible, s[:, lo:hi], -jnp.inf)
        mx = jnp.max(sd, axis=1, keepdims=True)
        if i > 0:
            so = s[:, :lo]
            mx = jnp.maximum(mx, jnp.max(so, axis=1, keepdims=True))
            p = jnp.concatenate([jnp.exp2(so - mx).astype(BF16), jnp.exp2(sd - mx).astype(BF16)], axis=1)
        else:
            p = jnp.exp2(sd - mx).astype(BF16)
        acc = jnp.dot(p, v_ref[0:hi, :], preferred_element_type=F32)
        o_ref[lo:hi, :] = (acc[:, :V_DIM] * (1.0 / acc[:, V_DIM:])).astype(BF16)


def _attention(q_cat, k_cat, v_ext, *, batch, seq, tq=256):
    m = q_cat.shape[0]
    tq = min(tq, seq)
    assert seq % tq == 0 and tq % ATTN_CHUNK == 0
    return pl.pallas_call(
        functools.partial(_attn_kernel, seq=seq, tq=tq),
        grid=(batch, A_HEADS),
        in_specs=[pl.BlockSpec((seq, QK_HEAD_PAD), lambda b, h: (b, h)),
                  pl.BlockSpec((seq, QK_HEAD_PAD), lambda b, h: (b, h)),
                  pl.BlockSpec((seq, V_HEAD_PAD), lambda b, h: (b, h))],
        out_specs=pl.BlockSpec((seq, V_DIM), lambda b, h: (b, h)),
        out_shape=jax.ShapeDtypeStruct((m, A_HEADS * V_DIM), BF16),
        compiler_params=_cparams(("parallel", "parallel")),
        name="mla_attention",
    )(q_cat, k_cat, v_ext)


def _rope_tables(seq):
    pos = jnp.arange(seq, dtype=F32)
    inv_freq = ROPE_THETA ** (-jnp.arange(0, ROPE_DIM, 2, dtype=F32) / ROPE_DIM)
    ang = pos[:, None] * inv_freq[None, :]
    pad = jnp.zeros((seq, LANES - ROPE_DIM), F32)
    cos, sin = jnp.cos(ang), jnp.sin(ang)
    return (jnp.concatenate([cos, cos, pad], axis=1), jnp.concatenate([sin, sin, pad], axis=1))


def kernel(x, a_w_in, a_b_gates, a_conv_w, a_conv_b, a_norm_w, a_w_out, kv_w_down, kv_norm_w,
           kv_w_up, b_w_dq, b_q_norm_w, b_w_uq, b_w_out, mlp_w1, mlp_w2, ln1_g, ln1_b, ln2_g, ln2_b):
    batch, seq, d = x.shape
    m = batch * seq
    xf = x.reshape(m, d)
    xb = None
    cos_t, sin_t = _rope_tables(seq)
    kv = None
    for layer in range(DEPTH):
        if layer < N_A_LAYERS:
            w_in = a_w_in[layer]
            lane_pad = ((0, 0), (0, LANES - M_HEADS))
            w_gate = jnp.concatenate(
                [jnp.pad(w_in[:, M_PROJ_WIDTH:M_PROJ_WIDTH + M_HEADS], lane_pad),
                 jnp.pad(w_in[:, M_PROJ_WIDTH + M_HEADS:], lane_pad)], axis=1).astype(BF16)
            proj, gates = _in_proj(xf, a_w_in, layer, w_gate)
            mix_in = _mlstm_scan(proj, gates, a_b_gates[layer], a_conv_w[layer], a_conv_b[layer],
                                 a_norm_w[layer], batch=batch, seq=seq)
            w_o = a_w_out[layer].astype(BF16)
        else:
            j = layer - N_A_LAYERS
            if xb is None:
                xb = xf.astype(BF16)
            if kv is None:
                wd = kv_w_down.astype(BF16)
                wd_r = jnp.pad(wd[:, KV_RANK:], ((0, 0), (0, LANES - ROPE_DIM)))
                kv = _shared_kv(xb, wd[:, :KV_RANK], wd_r, kv_norm_w, kv_w_up.astype(BF16),
                                cos_t, sin_t, seq=seq)
            w_uq = b_w_uq[j].reshape(Q_RANK, A_HEADS, NOPE_DIM + ROPE_DIM)
            w_uq = jnp.pad(w_uq, ((0, 0), (0, 0), (0, QK_HEAD_PAD - NOPE_DIM - ROPE_DIM)))
            w_uq = w_uq.reshape(Q_RANK, A_HEADS * QK_HEAD_PAD).astype(BF16)
            q_cat = _mla_q(xb, b_w_dq[j].astype(BF16), b_q_norm_w[j], w_uq, cos_t, sin_t, seq=seq)
            mix_in = _attention(q_cat, kv[0], kv[1], batch=batch, seq=seq)
            w_o = b_w_out[j].astype(BF16)
        xf, xb = _matmul_deepnorm(mix_in, w_o, xf, ln1_g[layer], ln1_b[layer],
                                  name=f"mix_out_deepnorm_{layer}")
        hid = _mlp_up(xb, mlp_w1, layer, name=f"mlp_up_{layer}")
        xf, xb = _matmul_deepnorm(hid, mlp_w2[layer].astype(BF16), xf, ln2_g[layer], ln2_b[layer],
                                  name=f"mlp_down_deepnorm_{layer}")
    return xf.reshape(batch, seq, d)
```
